```python
import math
import jax
import jax.numpy as jnp
from jax import lax
import numpy as np

D_MODEL = 1024
BATCH = 8
SEQ = 4096
DEPTH = 2
DEC_BATCH = 32
DEC_SEQ = 8
PAST_LEN = 16384
PAGE_SIZE = 128

N_AB = (DEPTH + 1) // 2
N_CD = DEPTH // 2
EPS = 1e-5
CONV_W = 4
D_FF = 4 * D_MODEL

A_BAND = 128
A_DILATIONS = (1, 4, 16)
A_WINDOWS = tuple(A_BAND * d for d in A_DILATIONS)
A_GROUPS = len(A_DILATIONS)
A_HEADS = D_MODEL // 128
A_HEAD_DIM = 64
A_WIDTH = A_HEADS * A_HEAD_DIM
B_INNER = D_MODEL
B_HEAD_DIM = 64
B_HEADS = B_INNER // B_HEAD_DIM
B_GROUPS = 2
B_HPG = B_HEADS // B_GROUPS
B_STATE = 128
B_CONV_DIM = B_INNER + 2 * B_GROUPS * B_STATE
SSD_CHUNK = 64
C_HEADS = 4
C_HEAD_DIM = 128
C_WIDTH = C_HEADS * C_HEAD_DIM
DELTA_CHUNK = 64
D_HEADS = 8
D_HEAD_DIM = 64
D_WIDTH = D_HEADS * D_HEAD_DIM
D_BIAS_INIT = -5.0
QBLK = 128

A_COLS = A_GROUPS * 3 * A_WIDTH
IN_AB = A_COLS + B_INNER + B_CONV_DIM + B_HEADS
AB_SPLITS = (A_COLS, A_COLS + B_INNER, A_COLS + B_INNER + B_CONV_DIM)
OUT_AB = A_WIDTH + B_INNER
IN_CD = 4 * C_WIDTH + 2 * C_HEADS + 3 * D_WIDTH
CD_SPLITS = (3 * C_WIDTH, 4 * C_WIDTH, 4 * C_WIDTH + C_HEADS, 4 * C_WIDTH + 2 * C_HEADS,
             4 * C_WIDTH + 2 * C_HEADS + D_WIDTH, 4 * C_WIDTH + 2 * C_HEADS + 2 * D_WIDTH)
OUT_CD = C_WIDTH + D_WIDTH
STATE_NAMES = ('a0', 'a1', 'a2', 'b_conv', 'b_ssm', 'c_conv', 'c_delta', 'd_k', 'd_v')

kernel_name = 'hybrid_dilated_ssd_deltanet_stickbreak_step'


def rms_norm(x, w):
    xf = x.astype(jnp.float32)
    y = xf * lax.rsqrt(jnp.mean(xf * xf, axis=-1, keepdims=True) + EPS)
    return (y * w.astype(jnp.float32)).astype(x.dtype)


def group_rms_norm(x, w, groups):
    shp = x.shape
    xg = x.astype(jnp.float32).reshape(*shp[:-1], groups, shp[-1] // groups)
    xg = xg * lax.rsqrt(jnp.mean(xg * xg, axis=-1, keepdims=True) + EPS)
    return (xg.reshape(shp) * w.astype(jnp.float32)).astype(x.dtype)


def l2_normalize(x):
    xf = x.astype(jnp.float32)
    return xf * lax.rsqrt(jnp.sum(xf * xf, axis=-1, keepdims=True) + 1e-6)


def causal_conv(x, prefix, w):
    L = x.shape[1]
    xp = jnp.concatenate([prefix.astype(x.dtype), x], axis=1)
    y = xp[:, :L] * w[0]
    for tap in range(1, CONV_W):
        y = y + xp[:, tap:tap + L] * w[tap]
    return y, xp[:, L:]


def band_attention(q, k, v):
    n, L, H, Dh = q.shape
    nb = -(-L // A_BAND)
    pad = nb * A_BAND - L
    def padded(t, front):
        return jnp.pad(t, ((0, 0), (front, pad), (0, 0), (0, 0)))
    qb = padded(q, 0).reshape(n, nb, A_BAND, H, Dh)
    kb = padded(k, A_BAND).reshape(n, nb + 1, A_BAND, H, Dh)
    vb = padded(v, A_BAND).reshape(n, nb + 1, A_BAND, H, Dh)
    kb = jnp.concatenate([kb[:, :-1], kb[:, 1:]], axis=2)
    vb = jnp.concatenate([vb[:, :-1], vb[:, 1:]], axis=2)
    s = jnp.einsum('nbqhd,nbkhd->nbhqk', qb, kb, preferred_element_type=jnp.float32) * (A_HEAD_DIM ** -0.5)
    qi = jnp.arange(A_BAND)[:, None]
    kj = jnp.arange(2 * A_BAND)[None, :]
    dist = A_BAND + qi - kj
    blk = jnp.arange(nb)[:, None, None]
    valid = (dist >= 0) & (dist <= A_BAND) & (blk * A_BAND + kj >= A_BAND)
    s = jnp.where(valid[None, :, None], s, -jnp.inf)
    lse = jax.nn.logsumexp(s, axis=-1)
    p = jnp.exp(s - lse[..., None]).astype(v.dtype)
    o = jnp.einsum('nbhqk,nbkhd->nbqhd', p, vb).reshape(n, nb * A_BAND, H, Dh)[:, :L]
    lse = lse.transpose(0, 1, 3, 2).reshape(n, nb * A_BAND, H)[:, :L]
    return o, lse


def dilated_prompt(q, k, v, d):
    b, S, H, Dh = q.shape
    def split(t):
        return t.reshape(b, S // d, d, H, Dh).transpose(0, 2, 1, 3, 4).reshape(b * d, S // d, H, Dh)
    o, lse = band_attention(split(q), split(k), split(v))
    o = o.reshape(b, d, S // d, H, Dh).transpose(0, 2, 1, 3, 4).reshape(b, S, H, Dh)
    lse = lse.reshape(b, d, S // d, H).transpose(0, 2, 1, 3).reshape(b, S, H)
    return o, lse


def dilated_sample(q, k_all, v_all, d, n_buf):
    L = q.shape[1]
    idx = n_buf + jnp.arange(L)[:, None] - d * jnp.arange(A_BAND + 1)[None, :]
    valid = idx >= 0
    idx = jnp.maximum(idx, 0)
    kg = k_all[:, idx]
    vg = v_all[:, idx]
    s = jnp.einsum('blhd,bljhd->blhj', q, kg, preferred_element_type=jnp.float32) * (A_HEAD_DIM ** -0.5)
    s = jnp.where(valid[None, :, None, :], s, -jnp.inf)
    lse = jax.nn.logsumexp(s, axis=-1)
    p = jnp.exp(s - lse[..., None]).astype(v_all.dtype)
    return jnp.einsum('blhj,bljhd->blhd', p, vg), lse


def ssd_scan(x, dt, a, bm, cm, h0):
    b, L, G, J, P = x.shape
    N = bm.shape[-1]
    c = math.gcd(L, SSD_CHUNK)
    nc = L // c
    f32 = jnp.float32
    x = x.astype(f32).reshape(b, nc, c, G, J, P)
    dt = dt.astype(f32).reshape(b, nc, c, G, J)
    bm = bm.astype(f32).reshape(b, nc, c, G, N)
    cm = cm.astype(f32).reshape(b, nc, c, G, N)
    acs = jnp.cumsum(dt * a.astype(f32), axis=2)
    causal = jnp.tril(jnp.ones((c, c), bool))[:, :, None, None]
    seg = acs[:, :, :, None] - acs[:, :, None, :]
    decay = jnp.exp(jnp.where(causal, seg, -jnp.inf))
    cb = jnp.einsum('bctgn,bcsgn->bctsg', cm, bm)
    w_ts = cb[..., None] * decay * dt[:, :, None]
    y_diag = jnp.einsum('bctsgj,bcsgjp->bctgjp', w_ts, x)
    decay_end = jnp.exp(acs[:, :, -1:] - acs) * dt
    chunk_states = jnp.einsum('bcsgn,bcsgj,bcsgjp->bcgjpn', bm, decay_end, x)
    chunk_decay = jnp.exp(acs[:, :, -1])
    def step(h, inp):
        st, dec = inp
        return h * dec[..., None, None] + st, h
    h_final, h_prev = lax.scan(step, h0.astype(f32),
                               (jnp.moveaxis(chunk_states, 1, 0), jnp.moveaxis(chunk_decay, 1, 0)))
    h_prev = jnp.moveaxis(h_prev, 0, 1)
    y_off = jnp.einsum('bctgn,bcgjpn->bctgjp', cm, h_prev) * jnp.exp(acs)[..., None]
    return (y_diag + y_off).reshape(b, L, G, J, P), h_final


def mamba2_mixer(z, xbc, dt_raw, conv_prefix, ssm0, conv_w, conv_b, dt_bias, a_log, d_skip, norm_w):
    b, L, _ = xbc.shape
    f32 = jnp.float32
    xbc, conv_state = causal_conv(xbc, conv_prefix, conv_w)
    xbc = jax.nn.silu(xbc + conv_b)
    xs, bm, cm = jnp.split(xbc, (B_INNER, B_INNER + B_GROUPS * B_STATE), axis=-1)
    xs = xs.reshape(b, L, B_GROUPS, B_HPG, B_HEAD_DIM)
    bm = bm.reshape(b, L, B_GROUPS, B_STATE)
    cm = cm.reshape(b, L, B_GROUPS, B_STATE)
    dt = jax.nn.softplus(dt_raw.astype(f32) + dt_bias.astype(f32)).reshape(b, L, B_GROUPS, B_HPG)
    a = -jnp.exp(a_log.astype(f32)).reshape(B_GROUPS, B_HPG)
    y, h = ssd_scan(xs, dt, a, bm, cm, ssm0.reshape(b, B_GROUPS, B_HPG, B_HEAD_DIM, B_STATE))
    y = y + d_skip.astype(f32).reshape(B_GROUPS, B_HPG, 1) * xs.astype(f32)
    y = y.reshape(b, L, B_INNER) * jax.nn.silu(z.astype(f32))
    y = group_rms_norm(y, norm_w, B_GROUPS)
    return y.astype(z.dtype), conv_state, h.reshape(b, B_HEADS, B_HEAD_DIM, B_STATE)


def gated_delta_chunked(q, k, v, log_alpha, beta, s0):
    b, L, H, Dk = q.shape
    Dv = v.shape[-1]
    c = math.gcd(L, DELTA_CHUNK)
    nc = L // c
    f32 = jnp.float32
    def blocks(t):
        return jnp.moveaxis(t.astype(f32).reshape(b, nc, c, H, *t.shape[3:]), 3, 2)
    q, k, v = blocks(q), blocks(k), blocks(v)
    la, bt = blocks(log_alpha), blocks(beta)
    g = jnp.cumsum(la, axis=-1)
    seg = g[..., :, None] - g[..., None, :]
    idx = jnp.arange(c)
    strict = idx[:, None] > idx[None, :]
    incl = idx[:, None] >= idx[None, :]
    kk = jnp.einsum('bchtd,bchsd->bchts', k, k)
    qk = jnp.einsum('bchtd,bchsd->bchts', q, k)
    a_mat = bt[..., :, None] * jnp.exp(jnp.where(strict, seg, -jnp.inf)) * kk
    p_mat = jnp.exp(jnp.where(incl, seg, -jnp.inf)) * qk
    rhs = jnp.concatenate([bt[..., None] * v, (bt * jnp.exp(g))[..., None] * k], axis=-1)
    sol = lax.linalg.triangular_solve(a_mat + jnp.eye(c, dtype=f32), rhs, left_side=True, lower=True,
                                      unit_diagonal=True)
    w_v, w_k = sol[..., :Dv], sol[..., Dv:]
    k_end = jnp.exp(g[..., -1:] - g)[..., None] * k
    g_last = jnp.exp(g[..., -1])
    q_dec = jnp.exp(g)[..., None] * q
    def step(s, inp):
        wv, wk, qd, pm, ke, gl = inp
        u = wv - jnp.einsum('bhtk,bhkv->bhtv', wk, s)
        o = jnp.einsum('bhtk,bhkv->bhtv', qd, s) + jnp.einsum('bhts,bhsv->bhtv', pm, u)
        s = gl[..., None, None] * s + jnp.einsum('bhsk,bhsv->bhkv', ke, u)
        return s, o
    xs = tuple(jnp.moveaxis(t, 1, 0) for t in (w_v, w_k, q_dec, p_mat, k_end, g_last))
    s_final, o = lax.scan(step, s0.astype(f32), xs)
    o = jnp.moveaxis(jnp.moveaxis(o, 0, 1), 3, 2).reshape(b, L, H, Dv)
    return o, s_final


def gated_delta_mixer(qkv, z, a_raw, b_raw, conv_prefix, s0, conv_w, dt_bias, a_log, norm_w):
    b, L, _ = qkv.shape
    f32 = jnp.float32
    qkv, conv_state = causal_conv(qkv, conv_prefix, conv_w)
    qkv = jax.nn.silu(qkv)
    q, k, v = (t.reshape(b, L, C_HEADS, C_HEAD_DIM) for t in jnp.split(qkv, 3, axis=-1))
    q = l2_normalize(q) * (C_HEAD_DIM ** -0.5)
    k = l2_normalize(k)
    log_alpha = -jnp.exp(a_log.astype(f32)) * jax.nn.softplus(a_raw.astype(f32) + dt_bias.astype(f32))
    beta = jax.nn.sigmoid(b_raw.astype(f32))
    o, s = gated_delta_chunked(q, k, v, log_alpha, beta, s0)
    o = rms_norm(o, norm_w) * jax.nn.silu(z.astype(f32)).reshape(b, L, C_HEADS, C_HEAD_DIM)
    return o.reshape(b, L, C_WIDTH).astype(z.dtype), conv_state, s


def stick_breaking_weights(z, mask):
    z = z.astype(jnp.float32)
    log_keep = jnp.where(mask, jax.nn.log_sigmoid(-z), 0.0)
    after = lax.cumsum(log_keep, axis=z.ndim - 1, reverse=True) - log_keep
    return jnp.where(mask, jnp.exp(jax.nn.log_sigmoid(z) + after), 0.0)


def stick_breaking_prompt(q, k, v, bias):
    b, S, H, Dh = q.shape
    nq = S // QBLK
    qb = q.reshape(b, nq, QBLK, H, Dh).transpose(1, 0, 2, 3, 4)
    kpos = jnp.arange(S)
    bias = bias.astype(jnp.float32)[None, :, None, None]
    def block(args):
        qblk, i = args
        z = jnp.einsum('bqhd,bkhd->bhqk', qblk, k, preferred_element_type=jnp.float32) * (D_HEAD_DIM ** -0.5)
        qpos = i * QBLK + jnp.arange(QBLK)
        w = stick_breaking_weights(z + bias, kpos[None, :] < qpos[:, None])
        return jnp.einsum('bhqk,bkhd->bqhd', w.astype(v.dtype), v)
    o = lax.map(block, (qb, jnp.arange(nq)))
    return o.transpose(1, 0, 2, 3, 4).reshape(b, S, H, Dh)


def stick_breaking_sample(q, k, v, k_past, v_past, bias):
    L = q.shape[1]
    P = k_past.shape[1]
    z = jnp.concatenate([
        jnp.einsum('blhd,bkhd->bhlk', q, k_past.astype(q.dtype), preferred_element_type=jnp.float32),
        jnp.einsum('blhd,bkhd->bhlk', q, k, preferred_element_type=jnp.float32)], axis=-1) * (D_HEAD_DIM ** -0.5)
    z = z + bias.astype(jnp.float32)[None, :, None, None]
    mask = jnp.concatenate([jnp.ones((L, P), bool), jnp.tril(jnp.ones((L, L), bool), -1)], axis=1)
    w = stick_breaking_weights(z, mask).astype(v.dtype)
    return (jnp.einsum('bhlk,bkhd->blhd', w[..., :P], v_past.astype(v.dtype))
            + jnp.einsum('bhlk,bkhd->blhd', w[..., P:], v))


def mixer_ab(h, w_in, conv_w, conv_b, dt_bias, a_log, d_skip, norm_w, w_out, conv_prefix, ssm0, windows):
    b, L, _ = h.shape
    a_cols, z, xbc, dt_raw = jnp.split(h @ w_in, AB_SPLITS, axis=-1)
    qkv = a_cols.reshape(b, L, A_GROUPS, 3, A_HEADS, A_HEAD_DIM)
    outs, lses, rows = [], [], []
    for g, d in enumerate(A_DILATIONS):
        q, k, v = qkv[:, :, g, 0], qkv[:, :, g, 1], qkv[:, :, g, 2]
        if windows is None:
            o, lse = dilated_prompt(q, k, v, d)
            keep = min(A_WINDOWS[g], L)
            rows.append(jnp.stack([k[:, L - keep:], v[:, L - keep:]], axis=1))
        else:
            buf = windows[g].astype(k.dtype)
            k_all = jnp.concatenate([buf[:, 0], k], axis=1)
            v_all = jnp.concatenate([buf[:, 1], v], axis=1)
            o, lse = dilated_sample(q, k_all, v_all, d, buf.shape[2])
            rows.append(jnp.stack([k, v], axis=1))
        outs.append(o)
        lses.append(lse)
    alpha = jax.nn.softmax(jnp.stack(lses), axis=0)
    y_a = jnp.einsum('gblh,gblhd->blhd', alpha.astype(h.dtype), jnp.stack(outs)).reshape(b, L, A_WIDTH)
    y_b, conv_state, ssm_state = mamba2_mixer(z, xbc, dt_raw, conv_prefix, ssm0, conv_w, conv_b, dt_bias,
                                              a_log, d_skip, norm_w)
    y = jnp.concatenate([y_a, y_b], axis=-1) @ w_out
    return y, rows, conv_state, ssm_state


def mixer_cd(h, w_in, conv_w, dt_bias, a_log, norm_w, w_out, sb_bias, conv_prefix, delta0, past_kv):
    b, L, _ = h.shape
    qkv_c, z_c, a_raw, b_raw, q_d, k_d, v_d = jnp.split(h @ w_in, CD_SPLITS, axis=-1)
    y_c, conv_state, delta_state = gated_delta_mixer(qkv_c, z_c, a_raw, b_raw, conv_prefix, delta0, conv_w,
                                                     dt_bias, a_log, norm_w)
    q_d, k_d, v_d = (t.reshape(b, L, D_HEADS, D_HEAD_DIM) for t in (q_d, k_d, v_d))
    if past_kv is None:
        y_d = stick_breaking_prompt(q_d, k_d, v_d, sb_bias)
    else:
        y_d = stick_breaking_sample(q_d, k_d, v_d, past_kv[0], past_kv[1], sb_bias)
    y = jnp.concatenate([y_c, y_d.reshape(b, L, D_WIDTH)], axis=-1) @ w_out
    return y, conv_state, delta_state, k_d, v_d


def squared_relu_mlp(h, w_up, w_down):
    return jnp.square(jax.nn.relu(h @ w_up)) @ w_down


def run_trunk(x, w, past):
    b, L, _ = x.shape
    new = {name: [] for name in STATE_NAMES}
    for layer in range(DEPTH):
        i = layer // 2
        h = rms_norm(x, w['norm_mix'][layer])
        if layer % 2 == 0:
            if past is None:
                windows = None
                conv0 = jnp.zeros((b, CONV_W - 1, B_CONV_DIM), x.dtype)
                ssm0 = jnp.zeros((b, B_HEADS, B_HEAD_DIM, B_STATE), jnp.float32)
            else:
                windows = (past['a0'][i], past['a1'][i], past['a2'][i])
                conv0 = past['b_conv'][i]
                ssm0 = past['b_ssm'][i]
            y, rows, conv_st, ssm_st = mixer_ab(h, w['w_in_ab'][i], w['conv_w_b'][i], w['conv_b_b'][i],
                                                w['dt_bias_b'][i], w['a_log_b'][i], w['d_skip_b'][i],
                                                w['norm_b'][i], w['w_out_ab'][i], conv0, ssm0, windows)
            for g in range(A_GROUPS):
                new['a%d' % g].append(rows[g])
            new['b_conv'].append(conv_st)
            new['b_ssm'].append(ssm_st)
        else:
            if past is None:
                conv0 = jnp.zeros((b, CONV_W - 1, 3 * C_WIDTH), x.dtype)
                delta0 = jnp.zeros((b, C_HEADS, C_HEAD_DIM, C_HEAD_DIM), jnp.float32)
                past_kv = None
            else:
                conv0 = past['c_conv'][i]
                delta0 = past['c_delta'][i]
                pt = past['page_table']
                k_past = past['d_k'][i][pt].reshape(b, -1, D_HEADS, D_HEAD_DIM)
                v_past = past['d_v'][i][pt].reshape(b, -1, D_HEADS, D_HEAD_DIM)
                past_kv = (k_past, v_past)
            y, conv_st, delta_st, k_rows, v_rows = mixer_cd(h, w['w_in_cd'][i], w['conv_w_c'][i],
                                                            w['dt_bias_c'][i], w['a_log_c'][i], w['norm_c'][i],
                                                            w['w_out_cd'][i], w['sb_bias_d'][i], conv0, delta0,
                                                            past_kv)
            new['c_conv'].append(conv_st)
            new['c_delta'].append(delta_st)
            new['d_k'].append(k_rows)
            new['d_v'].append(v_rows)
        x = x + y
        h = rms_norm(x, w['norm_mlp'][layer])
        x = x + squared_relu_mlp(h, w['w_up'][layer], w['w_down'][layer])
    out = {name: jnp.stack(vals) for name, vals in new.items()}
    return rms_norm(x, w['norm_f']), out


def setup_inputs(seed: int = 0) -> dict:
    key = jax.random.key(seed)
    f32 = jnp.float32
    counter = [0]
    def next_key():
        counter[0] += 1
        return jax.random.fold_in(key, counter[0])
    def normal(shape, scale=1.0):
        return jax.random.normal(next_key(), shape, f32) * scale
    def gain(shape):
        return 1.0 + normal(shape, 0.01)
    def log_uniform(shape, lo, hi):
        return jax.random.uniform(next_key(), shape, f32, math.log(lo), math.log(hi))
    def dt_bias(shape):
        dt = jnp.exp(log_uniform(shape, 1e-3, 1e-1))
        return dt + jnp.log(-jnp.expm1(-dt))
    n_pages = PAST_LEN // PAGE_SIZE
    n_pool = (DEC_BATCH * n_pages * 5) // 4
    perm = jax.random.permutation(next_key(), n_pool)
    page_table = perm[:DEC_BATCH * n_pages].reshape(DEC_BATCH, n_pages).astype(jnp.int32)
    win = [min(wd, PAST_LEN) for wd in A_WINDOWS]
    return {
        'x_prompt': normal((BATCH, SEQ, D_MODEL)),
        'x_sample': normal((DEC_BATCH, DEC_SEQ, D_MODEL)),
        'cache_a0': normal((N_AB, DEC_BATCH, 2, win[0], A_HEADS, A_HEAD_DIM)),
        'cache_a1': normal((N_AB, DEC_BATCH, 2, win[1], A_HEADS, A_HEAD_DIM)),
        'cache_a2': normal((N_AB, DEC_BATCH, 2, win[2], A_HEADS, A_HEAD_DIM)),
        'state_b_conv': normal((N_AB, DEC_BATCH, CONV_W - 1, B_CONV_DIM)),
        'state_b_ssm': normal((N_AB, DEC_BATCH, B_HEADS, B_HEAD_DIM, B_STATE), 0.1),
        'state_c_conv': normal((N_CD, DEC_BATCH, CONV_W - 1, 3 * C_WIDTH)),
        'state_c_delta': normal((N_CD, DEC_BATCH, C_HEADS, C_HEAD_DIM, C_HEAD_DIM), 0.1),
        'cache_d_k': normal((N_CD, n_pool, PAGE_SIZE, D_HEADS, D_HEAD_DIM)),
        'cache_d_v': normal((N_CD, n_pool, PAGE_SIZE, D_HEADS, D_HEAD_DIM)),
        'page_table': page_table,
        'w_in_ab': normal((N_AB, D_MODEL, IN_AB), D_MODEL ** -0.5),
        'conv_w_b': normal((N_AB, CONV_W, B_CONV_DIM), CONV_W ** -0.5),
        'conv_b_b': normal((N_AB, B_CONV_DIM), 0.01),
        'dt_bias_b': dt_bias((N_AB, B_HEADS)),
        'a_log_b': log_uniform((N_AB, B_HEADS), 1.0, 16.0),
        'd_skip_b': gain((N_AB, B_HEADS)),
        'norm_b': gain((N_AB, B_INNER)),
        'w_out_ab': normal((N_AB, OUT_AB, D_MODEL), OUT_AB ** -0.5),
        'w_in_cd': normal((N_CD, D_MODEL, IN_CD), D_MODEL ** -0.5),
        'conv_w_c': normal((N_CD, CONV_W, 3 * C_WIDTH), CONV_W ** -0.5),
        'dt_bias_c': dt_bias((N_CD, C_HEADS)),
        'a_log_c': log_uniform((N_CD, C_HEADS), 1.0, 16.0),
        'norm_c': gain((N_CD, C_HEAD_DIM)),
        'w_out_cd': normal((N_CD, OUT_CD, D_MODEL), OUT_CD ** -0.5),
        'sb_bias_d': D_BIAS_INIT + normal((N_CD, D_HEADS), 0.1),
        'norm_mix': gain((DEPTH, D_MODEL)),
        'norm_mlp': gain((DEPTH, D_MODEL)),
        'w_up': normal((DEPTH, D_MODEL, D_FF), D_MODEL ** -0.5),
        'w_down': normal((DEPTH, D_FF, D_MODEL), D_FF ** -0.5),
        'norm_f': gain((D_MODEL,)),
    }


def reference(x_prompt, x_sample, cache_a0, cache_a1, cache_a2, state_b_conv, state_b_ssm, state_c_conv,
              state_c_delta, cache_d_k, cache_d_v, page_table, w_in_ab, conv_w_b, conv_b_b, dt_bias_b, a_log_b,
              d_skip_b, norm_b, w_out_ab, w_in_cd, conv_w_c, dt_bias_c, a_log_c, norm_c, w_out_cd, sb_bias_d,
              norm_mix, norm_mlp, w_up, w_down, norm_f):
    weights = {
        'w_in_ab': w_in_ab, 'conv_w_b': conv_w_b, 'conv_b_b': conv_b_b, 'dt_bias_b': dt_bias_b,
        'a_log_b': a_log_b, 'd_skip_b': d_skip_b, 'norm_b': norm_b, 'w_out_ab': w_out_ab,
        'w_in_cd': w_in_cd, 'conv_w_c': conv_w_c, 'dt_bias_c': dt_bias_c, 'a_log_c': a_log_c,
        'norm_c': norm_c, 'w_out_cd': w_out_cd, 'sb_bias_d': sb_bias_d, 'norm_mix': norm_mix,
        'norm_mlp': norm_mlp, 'w_up': w_up, 'w_down': w_down, 'norm_f': norm_f,
    }
    past = {
        'a0': cache_a0, 'a1': cache_a1, 'a2': cache_a2, 'b_conv': state_b_conv, 'b_ssm': state_b_ssm,
        'c_conv': state_c_conv, 'c_delta': state_c_delta, 'd_k': cache_d_k, 'd_v': cache_d_v,
        'page_table': page_table,
    }
    y_prompt, sp = run_trunk(x_prompt, weights, None)
    y_sample, ss = run_trunk(x_sample, weights, past)
    return (y_prompt, y_sample,
            sp['a0'], ss['a0'], sp['a1'], ss['a1'], sp['a2'], ss['a2'],
            sp['b_conv'], ss['b_conv'], sp['b_ssm'], ss['b_ssm'],
            sp['c_conv'], ss['c_conv'], sp['c_delta'], ss['c_delta'],
            sp['d_k'], ss['d_k'], sp['d_v'], ss['d_v'])
```

```python
import functools
import math

import jax
import jax.numpy as jnp
from jax import lax
from jax.experimental import pallas as pl
from jax.experimental.pallas import tpu as pltpu

F32 = jnp.float32
BF16 = jnp.bfloat16
EPS = 1e-5

V7X_VMEM_BYTES = 64 * 1024 * 1024
VMEM_LIMIT = V7X_VMEM_BYTES - 8 * 1024 * 1024
LANES = 128

D_MODEL = 1024
CONV_W = 4
A_BAND = 128
A_DILATIONS = (1, 4, 16)
A_WINDOWS = tuple(A_BAND * d for d in A_DILATIONS)
A_GROUPS = 3
A_HEADS = 8
A_HEAD_DIM = 64
A_WIDTH = A_HEADS * A_HEAD_DIM
B_INNER = D_MODEL
B_HEAD_DIM = 64
B_HEADS = B_INNER // B_HEAD_DIM
B_GROUPS = 2
B_HPG = B_HEADS // B_GROUPS
B_STATE = 128
B_CONV_DIM = B_INNER + 2 * B_GROUPS * B_STATE
SSD_CHUNK = 64
C_HEADS = 4
C_HEAD_DIM = 128
C_WIDTH = C_HEADS * C_HEAD_DIM
DELTA_CHUNK = 64
D_HEADS = 8
D_HEAD_DIM = 64
D_WIDTH = D_HEADS * D_HEAD_DIM
QBLK = 128
A_COLS = A_GROUPS * 3 * A_WIDTH
AB_MAIN = A_COLS + B_INNER + B_CONV_DIM
CD_MAIN = 4 * C_WIDTH
CD_D0 = 4 * C_WIDTH + 2 * C_HEADS


def _params(sem):
    return pltpu.CompilerParams(dimension_semantics=sem, vmem_limit_bytes=VMEM_LIMIT)


def _rms(x, w):
    return x * lax.rsqrt(jnp.mean(x * x, axis=-1, keepdims=True) + EPS) * w


def _proj_kernel(x_ref, nw_ref, w_ref, ws_ref, out_ref, small_ref, h_ref):
    @pl.when(pl.program_id(1) == 0)
    def _():
        h = _rms(x_ref[...], nw_ref[...]).astype(BF16)
        h_ref[...] = h
        small_ref[...] = jnp.dot(h, ws_ref[...], preferred_element_type=F32)

    out_ref[...] = jnp.dot(h_ref[...], w_ref[...], preferred_element_type=F32)


def rms_project(x, norm_w, w_main, w_small, tn, tm):
    m, d = x.shape
    n = w_main.shape[1]
    assert m % tm == 0 and n % tn == 0
    return pl.pallas_call(
        _proj_kernel,
        grid=(m // tm, n // tn),
        in_specs=[
            pl.BlockSpec((tm, d), lambda i, j: (i, 0)),
            pl.BlockSpec((1, d), lambda i, j: (0, 0)),
            pl.BlockSpec((d, tn), lambda i, j: (0, j)),
            pl.BlockSpec((d, LANES), lambda i, j: (0, 0)),
        ],
        out_specs=[pl.BlockSpec((tm, tn), lambda i, j: (i, j)), pl.BlockSpec((tm, LANES), lambda i, j: (i, 0))],
        out_shape=[jax.ShapeDtypeStruct((m, n), F32), jax.ShapeDtypeStruct((m, LANES), F32)],
        scratch_shapes=[pltpu.VMEM((tm, d), BF16)],
        compiler_params=_params(("arbitrary", "arbitrary")),
        name="rms_project",
    )(x, norm_w.reshape(1, d), w_main, w_small)


def _outproj_kernel(n_piece, *refs):
    x_ref, w_ref, out_ref = refs[n_piece], refs[n_piece + 1], refs[n_piece + 2]
    acc = x_ref[...]
    off = 0
    for p_ref in refs[:n_piece]:
        k = p_ref.shape[1]
        acc = acc + jnp.dot(p_ref[...].astype(BF16), w_ref[off:off + k, :], preferred_element_type=F32)
        off += k
    out_ref[...] = acc


def out_project(pieces, x, w, tm):
    m, d = x.shape
    in_specs = [pl.BlockSpec((tm, a.shape[1]), lambda i: (i, 0)) for a in pieces]
    in_specs += [pl.BlockSpec((tm, d), lambda i: (i, 0)), pl.BlockSpec(w.shape, lambda i: (0, 0))]
    return pl.pallas_call(
        functools.partial(_outproj_kernel, len(pieces)),
        grid=(m // tm,),
        in_specs=in_specs,
        out_specs=pl.BlockSpec((tm, d), lambda i: (i, 0)),
        out_shape=jax.ShapeDtypeStruct((m, d), F32),
        compiler_params=_params(("arbitrary",)),
        name="out_project",
    )(*pieces, x, w)


def _mlp_kernel(final, x_ref, nw_ref, wu_ref, wd_ref, nf_ref, out_ref, h_ref, acc_ref):
    j = pl.program_id(1)

    @pl.when(j == 0)
    def _():
        h_ref[...] = _rms(x_ref[...], nw_ref[...]).astype(BF16)
        acc_ref[...] = jnp.zeros_like(acc_ref)

    u = jnp.dot(h_ref[...], wu_ref[...], preferred_element_type=F32)
    u = jnp.maximum(u, 0.0)
    acc_ref[...] += jnp.dot((u * u).astype(BF16), wd_ref[...], preferred_element_type=F32)

    @pl.when(j == pl.num_programs(1) - 1)
    def _():
        y = x_ref[...] + acc_ref[...]
        if final:
            y = _rms(y, nf_ref[...])
        out_ref[...] = y


def mlp_block(x, norm_w, w_up, w_down, norm_f, final, tm, tf):
    m, d = x.shape
    dff = w_up.shape[1]
    return pl.pallas_call(
        functools.partial(_mlp_kernel, final),
        grid=(m // tm, dff // tf),
        in_specs=[
            pl.BlockSpec((tm, d), lambda i, j: (i, 0)),
            pl.BlockSpec((1, d), lambda i, j: (0, 0)),
            pl.BlockSpec((d, tf), lambda i, j: (0, j)),
            pl.BlockSpec((tf, d), lambda i, j: (j, 0)),
            pl.BlockSpec((1, d), lambda i, j: (0, 0)),
        ],
        out_specs=pl.BlockSpec((tm, d), lambda i, j: (i, 0)),
        out_shape=jax.ShapeDtypeStruct((m, d), F32),
        scratch_shapes=[pltpu.VMEM((tm, d), BF16), pltpu.VMEM((tm, d), F32)],
        compiler_params=_params(("arbitrary", "arbitrary")),
        name="mlp_block",
    )(x, norm_w.reshape(1, d), w_up, w_down, norm_f.reshape(1, d))


def _rms_norm(x, w):
    xf = x.astype(F32)
    y = xf * lax.rsqrt(jnp.mean(xf * xf, axis=-1, keepdims=True) + EPS)
    return (y * w.astype(F32)).astype(x.dtype)


def _group_rms_norm(x, w, groups):
    shp = x.shape
    xg = x.astype(F32).reshape(*shp[:-1], groups, shp[-1] // groups)
    xg = xg * lax.rsqrt(jnp.mean(xg * xg, axis=-1, keepdims=True) + EPS)
    return (xg.reshape(shp) * w.astype(F32)).astype(x.dtype)


def _l2_normalize(x):
    xf = x.astype(F32)
    return xf * lax.rsqrt(jnp.sum(xf * xf, axis=-1, keepdims=True) + 1e-6)


def _causal_conv(x, prefix, w):
    seq = x.shape[1]
    xp = jnp.concatenate([prefix.astype(x.dtype), x], axis=1)
    y = xp[:, :seq] * w[0]
    for tap in range(1, CONV_W):
        y = y + xp[:, tap:tap + seq] * w[tap]
    return y, xp[:, seq:]


def _band_attention(q, k, v):
    n, seq, heads, dh = q.shape
    nb = -(-seq // A_BAND)
    pad = nb * A_BAND - seq

    def padded(t, front):
        return jnp.pad(t, ((0, 0), (front, pad), (0, 0), (0, 0)))

    qb = padded(q, 0).reshape(n, nb, A_BAND, heads, dh)
    kb = padded(k, A_BAND).reshape(n, nb + 1, A_BAND, heads, dh)
    vb = padded(v, A_BAND).reshape(n, nb + 1, A_BAND, heads, dh)
    kb = jnp.concatenate([kb[:, :-1], kb[:, 1:]], axis=2)
    vb = jnp.concatenate([vb[:, :-1], vb[:, 1:]], axis=2)
    s = jnp.einsum('nbqhd,nbkhd->nbhqk', qb, kb, preferred_element_type=F32) * (A_HEAD_DIM ** -0.5)
    qi = jnp.arange(A_BAND)[:, None]
    kj = jnp.arange(2 * A_BAND)[None, :]
    dist = A_BAND + qi - kj
    blk = jnp.arange(nb)[:, None, None]
    valid = (dist >= 0) & (dist <= A_BAND) & (blk * A_BAND + kj >= A_BAND)
    s = jnp.where(valid[None, :, None], s, -jnp.inf)
    lse = jax.nn.logsumexp(s, axis=-1)
    p = jnp.exp(s - lse[..., None]).astype(v.dtype)
    o = jnp.einsum('nbhqk,nbkhd->nbqhd', p, vb).reshape(n, nb * A_BAND, heads, dh)[:, :seq]
    lse = lse.transpose(0, 1, 3, 2).reshape(n, nb * A_BAND, heads)[:, :seq]
    return o, lse


def _dilated_prompt(q, k, v, d):
    b, seq, heads, dh = q.shape

    def split(t):
        return t.reshape(b, seq // d, d, heads, dh).transpose(0, 2, 1, 3, 4).reshape(b * d, seq // d, heads, dh)

    o, lse = _band_attention(split(q), split(k), split(v))
    o = o.reshape(b, d, seq // d, heads, dh).transpose(0, 2, 1, 3, 4).reshape(b, seq, heads, dh)
    lse = lse.reshape(b, d, seq // d, heads).transpose(0, 2, 1, 3).reshape(b, seq, heads)
    return o, lse


def _dilated_sample(q, k_all, v_all, d, n_buf):
    seq = q.shape[1]
    idx = n_buf + jnp.arange(seq)[:, None] - d * jnp.arange(A_BAND + 1)[None, :]
    valid = idx >= 0
    idx = jnp.maximum(idx, 0)
    kg = k_all[:, idx]
    vg = v_all[:, idx]
    s = jnp.einsum('blhd,bljhd->blhj', q, kg, preferred_element_type=F32) * (A_HEAD_DIM ** -0.5)
    s = jnp.where(valid[None, :, None, :], s, -jnp.inf)
    lse = jax.nn.logsumexp(s, axis=-1)
    p = jnp.exp(s - lse[..., None]).astype(v_all.dtype)
    return jnp.einsum('blhj,bljhd->blhd', p, vg), lse


def _ssd_scan(x, dt, a, bm, cm, h0):
    b, seq, groups, hpg, pdim = x.shape
    n = bm.shape[-1]
    c = math.gcd(seq, SSD_CHUNK)
    nc = seq // c
    x = x.astype(F32).reshape(b, nc, c, groups, hpg, pdim)
    dt = dt.astype(F32).reshape(b, nc, c, groups, hpg)
    bm = bm.astype(F32).reshape(b, nc, c, groups, n)
    cm = cm.astype(F32).reshape(b, nc, c, groups, n)
    acs = jnp.cumsum(dt * a.astype(F32), axis=2)
    causal = jnp.tril(jnp.ones((c, c), bool))[:, :, None, None]
    seg = acs[:, :, :, None] - acs[:, :, None, :]
    decay = jnp.exp(jnp.where(causal, seg, -jnp.inf))
    cb = jnp.einsum('bctgn,bcsgn->bctsg', cm, bm)
    w_ts = cb[..., None] * decay * dt[:, :, None]
    y_diag = jnp.einsum('bctsgj,bcsgjp->bctgjp', w_ts, x)
    decay_end = jnp.exp(acs[:, :, -1:] - acs) * dt
    chunk_states = jnp.einsum('bcsgn,bcsgj,bcsgjp->bcgjpn', bm, decay_end, x)
    chunk_decay = jnp.exp(acs[:, :, -1])

    def step(h, inp):
        st, dec = inp
        return h * dec[..., None, None] + st, h

    h_final, h_prev = lax.scan(step, h0.astype(F32),
                               (jnp.moveaxis(chunk_states, 1, 0), jnp.moveaxis(chunk_decay, 1, 0)))
    h_prev = jnp.moveaxis(h_prev, 0, 1)
    y_off = jnp.einsum('bctgn,bcgjpn->bctgjp', cm, h_prev) * jnp.exp(acs)[..., None]
    return (y_diag + y_off).reshape(b, seq, groups, hpg, pdim), h_final


def _mamba2_mixer(z, xbc, dt_raw, conv_prefix, ssm0, conv_w, conv_b, dt_bias, a_log, d_skip, norm_w):
    b, seq, _ = xbc.shape
    xbc, conv_state = _causal_conv(xbc, conv_prefix, conv_w)
    xbc = jax.nn.silu(xbc + conv_b)
    xs, bm, cm = jnp.split(xbc, (B_INNER, B_INNER + B_GROUPS * B_STATE), axis=-1)
    xs = xs.reshape(b, seq, B_GROUPS, B_HPG, B_HEAD_DIM)
    bm = bm.reshape(b, seq, B_GROUPS, B_STATE)
    cm = cm.reshape(b, seq, B_GROUPS, B_STATE)
    dt = jax.nn.softplus(dt_raw.astype(F32) + dt_bias.astype(F32)).reshape(b, seq, B_GROUPS, B_HPG)
    a = -jnp.exp(a_log.astype(F32)).reshape(B_GROUPS, B_HPG)
    y, h = _ssd_scan(xs, dt, a, bm, cm, ssm0.reshape(b, B_GROUPS, B_HPG, B_HEAD_DIM, B_STATE))
    y = y + d_skip.astype(F32).reshape(B_GROUPS, B_HPG, 1) * xs.astype(F32)
    y = y.reshape(b, seq, B_INNER) * jax.nn.silu(z.astype(F32))
    y = _group_rms_norm(y, norm_w, B_GROUPS)
    return y.astype(z.dtype), conv_state, h.reshape(b, B_HEADS, B_HEAD_DIM, B_STATE)


def _gated_delta_chunked(q, k, v, log_alpha, beta, s0):
    b, seq, heads, dk = q.shape
    dv = v.shape[-1]
    c = math.gcd(seq, DELTA_CHUNK)
    nc = seq // c

    def blocks(t):
        return jnp.moveaxis(t.astype(F32).reshape(b, nc, c, heads, *t.shape[3:]), 3, 2)

    q, k, v = blocks(q), blocks(k), blocks(v)
    la, bt = blocks(log_alpha), blocks(beta)
    g = jnp.cumsum(la, axis=-1)
    seg = g[..., :, None] - g[..., None, :]
    idx = jnp.arange(c)
    strict = idx[:, None] > idx[None, :]
    incl = idx[:, None] >= idx[None, :]
    kk = jnp.einsum('bchtd,bchsd->bchts', k, k)
    qk = jnp.einsum('bchtd,bchsd->bchts', q, k)
    a_mat = bt[..., :, None] * jnp.exp(jnp.where(strict, seg, -jnp.inf)) * kk
    p_mat = jnp.exp(jnp.where(incl, seg, -jnp.inf)) * qk
    rhs = jnp.concatenate([bt[..., None] * v, (bt * jnp.exp(g))[..., None] * k], axis=-1)
    sol = lax.linalg.triangular_solve(a_mat + jnp.eye(c, dtype=F32), rhs, left_side=True, lower=True,
                                      unit_diagonal=True)
    w_v, w_k = sol[..., :dv], sol[..., dv:]
    k_end = jnp.exp(g[..., -1:] - g)[..., None] * k
    g_last = jnp.exp(g[..., -1])
    q_dec = jnp.exp(g)[..., None] * q

    def step(s, inp):
        wv, wk, qd, pm, ke, gl = inp
        u = wv - jnp.einsum('bhtk,bhkv->bhtv', wk, s)
        o = jnp.einsum('bhtk,bhkv->bhtv', qd, s) + jnp.einsum('bhts,bhsv->bhtv', pm, u)
        s = gl[..., None, None] * s + jnp.einsum('bhsk,bhsv->bhkv', ke, u)
        return s, o

    xs = tuple(jnp.moveaxis(t, 1, 0) for t in (w_v, w_k, q_dec, p_mat, k_end, g_last))
    s_final, o = lax.scan(step, s0.astype(F32), xs)
    o = jnp.moveaxis(jnp.moveaxis(o, 0, 1), 3, 2).reshape(b, seq, heads, dv)
    return o, s_final


def _gated_delta_mixer(qkv, z, a_raw, b_raw, conv_prefix, s0, conv_w, dt_bias, a_log, norm_w):
    b, seq, _ = qkv.shape
    qkv, conv_state = _causal_conv(qkv, conv_prefix, conv_w)
    qkv = jax.nn.silu(qkv)
    q, k, v = (t.reshape(b, seq, C_HEADS, C_HEAD_DIM) for t in jnp.split(qkv, 3, axis=-1))
    q = _l2_normalize(q) * (C_HEAD_DIM ** -0.5)
    k = _l2_normalize(k)
    log_alpha = -jnp.exp(a_log.astype(F32)) * jax.nn.softplus(a_raw.astype(F32) + dt_bias.astype(F32))
    beta = jax.nn.sigmoid(b_raw.astype(F32))
    o, s = _gated_delta_chunked(q, k, v, log_alpha, beta, s0)
    o = _rms_norm(o, norm_w) * jax.nn.silu(z.astype(F32)).reshape(b, seq, C_HEADS, C_HEAD_DIM)
    return o.reshape(b, seq, C_WIDTH).astype(z.dtype), conv_state, s


def _stick_breaking_weights(z, mask):
    z = z.astype(F32)
    log_keep = jnp.where(mask, jax.nn.log_sigmoid(-z), 0.0)
    after = lax.cumsum(log_keep, axis=z.ndim - 1, reverse=True) - log_keep
    return jnp.where(mask, jnp.exp(jax.nn.log_sigmoid(z) + after), 0.0)


def _stick_breaking_prompt(q, k, v, bias):
    b, seq, heads, dh = q.shape
    nq = seq // QBLK
    qb = q.reshape(b, nq, QBLK, heads, dh).transpose(1, 0, 2, 3, 4)
    kpos = jnp.arange(seq)
    bias = bias.astype(F32)[None, :, None, None]

    def block(args):
        qblk, i = args
        z = jnp.einsum('bqhd,bkhd->bhqk', qblk, k, preferred_element_type=F32) * (D_HEAD_DIM ** -0.5)
        qpos = i * QBLK + jnp.arange(QBLK)
        w = _stick_breaking_weights(z + bias, kpos[None, :] < qpos[:, None])
        return jnp.einsum('bhqk,bkhd->bqhd', w.astype(v.dtype), v)

    o = lax.map(block, (qb, jnp.arange(nq)))
    return o.transpose(1, 0, 2, 3, 4).reshape(b, seq, heads, dh)


def _stick_breaking_sample(q, k, v, k_past, v_past, bias):
    seq = q.shape[1]
    past = k_past.shape[1]
    z = jnp.concatenate([
        jnp.einsum('blhd,bkhd->bhlk', q, k_past.astype(q.dtype), preferred_element_type=F32),
        jnp.einsum('blhd,bkhd->bhlk', q, k, preferred_element_type=F32)], axis=-1) * (D_HEAD_DIM ** -0.5)
    z = z + bias.astype(F32)[None, :, None, None]
    mask = jnp.concatenate([jnp.ones((seq, past), bool), jnp.tril(jnp.ones((seq, seq), bool), -1)], axis=1)
    w = _stick_breaking_weights(z, mask).astype(v.dtype)
    return (jnp.einsum('bhlk,bkhd->blhd', w[..., :past], v_past.astype(v.dtype))
            + jnp.einsum('bhlk,bkhd->blhd', w[..., past:], v))


def _pad_lanes(w):
    return jnp.pad(w, ((0, 0), (0, LANES - w.shape[1]))).astype(BF16)


def _run_trunk(x3, w, past):
    b, seq, d = x3.shape
    m = b * seq
    tm = min(1024, m)
    tm_out = min(512, m)
    x = x3.reshape(m, d)
    new = {}

    w_in = w['w_in_ab'][0]
    proj, small = rms_project(x, w['norm_mix'][0], w_in[:, :AB_MAIN].astype(BF16), _pad_lanes(w_in[:, AB_MAIN:]),
                              1024, tm)
    proj = proj.reshape(b, seq, AB_MAIN)
    qkv = proj[..., :A_COLS].reshape(b, seq, A_GROUPS, 3, A_HEADS, A_HEAD_DIM)
    z = proj[..., A_COLS:A_COLS + B_INNER]
    xbc = proj[..., A_COLS + B_INNER:]
    dt_raw = small[:, :B_HEADS].reshape(b, seq, B_HEADS)
    if past is None:
        conv0 = jnp.zeros((b, CONV_W - 1, B_CONV_DIM), F32)
        ssm0 = jnp.zeros((b, B_HEADS, B_HEAD_DIM, B_STATE), F32)
    else:
        conv0, ssm0 = past['b_conv'][0], past['b_ssm'][0]
    outs, lses = [], []
    for g, dil in enumerate(A_DILATIONS):
        q, k, v = qkv[:, :, g, 0], qkv[:, :, g, 1], qkv[:, :, g, 2]
        if past is None:
            o, lse = _dilated_prompt(q, k, v, dil)
            keep = min(A_WINDOWS[g], seq)
            new['a%d' % g] = jnp.stack([k[:, seq - keep:], v[:, seq - keep:]], axis=1)[None]
        else:
            buf = past['a%d' % g][0]
            k_all = jnp.concatenate([buf[:, 0], k], axis=1)
            v_all = jnp.concatenate([buf[:, 1], v], axis=1)
            o, lse = _dilated_sample(q, k_all, v_all, dil, buf.shape[2])
            new['a%d' % g] = jnp.stack([k, v], axis=1)[None]
        outs.append(o)
        lses.append(lse)
    alpha = jax.nn.softmax(jnp.stack(lses), axis=0)
    y_a = jnp.einsum('gblh,gblhd->blhd', alpha, jnp.stack(outs)).reshape(m, A_WIDTH)
    y_b, conv_st, ssm_st = _mamba2_mixer(z, xbc, dt_raw, conv0, ssm0, w['conv_w_b'][0], w['conv_b_b'][0],
                                         w['dt_bias_b'][0], w['a_log_b'][0], w['d_skip_b'][0], w['norm_b'][0])
    new['b_conv'], new['b_ssm'] = conv_st[None], ssm_st[None]
    x = out_project([y_a, y_b.reshape(m, B_INNER)], x, w['w_out_ab'][0].astype(BF16), tm_out)
    x = mlp_block(x, w['norm_mlp'][0], w['w_up'][0].astype(BF16), w['w_down'][0].astype(BF16), w['norm_f'],
                  False, tm, 512)

    w_in = w['w_in_cd'][0]
    w_main = jnp.concatenate([w_in[:, :CD_MAIN], w_in[:, CD_D0:]], axis=1).astype(BF16)
    proj, small = rms_project(x, w['norm_mix'][1], w_main, _pad_lanes(w_in[:, CD_MAIN:CD_D0]), 512, tm)
    proj = proj.reshape(b, seq, CD_MAIN + 3 * D_WIDTH)
    qkv_c = proj[..., :3 * C_WIDTH]
    z_c = proj[..., 3 * C_WIDTH:CD_MAIN]
    q_d, k_d, v_d = (proj[..., CD_MAIN + i * D_WIDTH:CD_MAIN + (i + 1) * D_WIDTH].reshape(b, seq, D_HEADS, D_HEAD_DIM)
                     for i in range(3))
    a_raw = small[:, :C_HEADS].reshape(b, seq, C_HEADS)
    b_raw = small[:, C_HEADS:2 * C_HEADS].reshape(b, seq, C_HEADS)
    if past is None:
        conv0 = jnp.zeros((b, CONV_W - 1, 3 * C_WIDTH), F32)
        delta0 = jnp.zeros((b, C_HEADS, C_HEAD_DIM, C_HEAD_DIM), F32)
    else:
        conv0, delta0 = past['c_conv'][0], past['c_delta'][0]
    y_c, conv_st, delta_st = _gated_delta_mixer(qkv_c, z_c, a_raw, b_raw, conv0, delta0, w['conv_w_c'][0],
                                                w['dt_bias_c'][0], w['a_log_c'][0], w['norm_c'][0])
    if past is None:
        y_d = _stick_breaking_prompt(q_d, k_d, v_d, w['sb_bias_d'][0])
    else:
        pt = past['page_table']
        k_past = past['d_k'][0][pt].reshape(b, -1, D_HEADS, D_HEAD_DIM)
        v_past = past['d_v'][0][pt].reshape(b, -1, D_HEADS, D_HEAD_DIM)
        y_d = _stick_breaking_sample(q_d, k_d, v_d, k_past, v_past, w['sb_bias_d'][0])
    new['c_conv'], new['c_delta'] = conv_st[None], delta_st[None]
    new['d_k'], new['d_v'] = k_d[None], v_d[None]
    x = out_project([y_c.reshape(m, C_WIDTH), y_d.reshape(m, D_WIDTH)], x, w['w_out_cd'][0].astype(BF16), tm_out)
    x = mlp_block(x, w['norm_mlp'][1], w['w_up'][1].astype(BF16), w['w_down'][1].astype(BF16), w['norm_f'],
                  True, tm, 512)
    return x.reshape(b, seq, d), new


def kernel(x_prompt, x_sample, cache_a0, cache_a1, cache_a2, state_b_conv, state_b_ssm, state_c_conv,
           state_c_delta, cache_d_k, cache_d_v, page_table, w_in_ab, conv_w_b, conv_b_b, dt_bias_b, a_log_b,
           d_skip_b, norm_b, w_out_ab, w_in_cd, conv_w_c, dt_bias_c, a_log_c, norm_c, w_out_cd, sb_bias_d,
           norm_mix, norm_mlp, w_up, w_down, norm_f):
    assert w_in_ab.shape[0] == 1 and w_in_cd.shape[0] == 1 and w_up.shape[0] == 2
    weights = {
        'w_in_ab': w_in_ab, 'conv_w_b': conv_w_b, 'conv_b_b': conv_b_b, 'dt_bias_b': dt_bias_b,
        'a_log_b': a_log_b, 'd_skip_b': d_skip_b, 'norm_b': norm_b, 'w_out_ab': w_out_ab,
        'w_in_cd': w_in_cd, 'conv_w_c': conv_w_c, 'dt_bias_c': dt_bias_c, 'a_log_c': a_log_c,
        'norm_c': norm_c, 'w_out_cd': w_out_cd, 'sb_bias_d': sb_bias_d, 'norm_mix': norm_mix,
        'norm_mlp': norm_mlp, 'w_up': w_up, 'w_down': w_down, 'norm_f': norm_f,
    }
    past = {
        'a0': cache_a0, 'a1': cache_a1, 'a2': cache_a2, 'b_conv': state_b_conv, 'b_ssm': state_b_ssm,
        'c_conv': state_c_conv, 'c_delta': state_c_delta, 'd_k': cache_d_k, 'd_v': cache_d_v,
        'page_table': page_table,
    }
    y_prompt, sp = _run_trunk(x_prompt, weights, None)
    y_sample, ss = _run_trunk(x_sample, weights, past)
    return (y_prompt, y_sample,
            sp['a0'], ss['a0'], sp['a1'], ss['a1'], sp['a2'], ss['a2'],
            sp['b_conv'], ss['b_conv'], sp['b_ssm'], ss['b_ssm'],
            sp['c_conv'], ss['c_conv'], sp['c_delta'], ss['c_delta'],
            sp['d_k'], ss['d_k'], sp['d_v'], ss['d_v'])
```

```python
import functools
import math

import jax
import jax.numpy as jnp
from jax import lax
from jax.experimental import pallas as pl
from jax.experimental.pallas import tpu as pltpu

F32 = jnp.float32
BF16 = jnp.bfloat16
EPS = 1e-5

V7X_VMEM_BYTES = 64 * 1024 * 1024
VMEM_LIMIT = V7X_VMEM_BYTES - 8 * 1024 * 1024
LANES = 128

D_MODEL = 1024
CONV_W = 4
A_BAND = 128
A_DILATIONS = (1, 4, 16)
A_WINDOWS = tuple(A_BAND * d for d in A_DILATIONS)
A_GROUPS = 3
A_HEADS = 8
A_HEAD_DIM = 64
A_WIDTH = A_HEADS * A_HEAD_DIM
B_INNER = D_MODEL
B_HEAD_DIM = 64
B_HEADS = B_INNER // B_HEAD_DIM
B_GROUPS = 2
B_HPG = B_HEADS // B_GROUPS
B_STATE = 128
B_CONV_DIM = B_INNER + 2 * B_GROUPS * B_STATE
SSD_CHUNK = 64
C_HEADS = 4
C_HEAD_DIM = 128
C_WIDTH = C_HEADS * C_HEAD_DIM
DELTA_CHUNK = 64
D_HEADS = 8
D_HEAD_DIM = 64
D_WIDTH = D_HEADS * D_HEAD_DIM
QBLK = 128
A_COLS = A_GROUPS * 3 * A_WIDTH
AB_MAIN = A_COLS + B_INNER + B_CONV_DIM
CD_MAIN = 4 * C_WIDTH
CD_D0 = 4 * C_WIDTH + 2 * C_HEADS


def _params(sem):
    return pltpu.CompilerParams(dimension_semantics=sem, vmem_limit_bytes=VMEM_LIMIT)


def _rms(x, w):
    return x * lax.rsqrt(jnp.mean(x * x, axis=-1, keepdims=True) + EPS) * w


def _proj_kernel(x_ref, nw_ref, w_ref, ws_ref, out_ref, small_ref, h_ref):
    @pl.when(pl.program_id(1) == 0)
    def _():
        h = _rms(x_ref[...], nw_ref[...]).astype(BF16)
        h_ref[...] = h
        small_ref[...] = jnp.dot(h, ws_ref[...], preferred_element_type=F32)

    out_ref[...] = jnp.dot(h_ref[...], w_ref[...], preferred_element_type=F32)


def rms_project(x, norm_w, w_main, w_small, tn, tm):
    m, d = x.shape
    n = w_main.shape[1]
    assert m % tm == 0 and n % tn == 0
    return pl.pallas_call(
        _proj_kernel,
        grid=(m // tm, n // tn),
        in_specs=[
            pl.BlockSpec((tm, d), lambda i, j: (i, 0)),
            pl.BlockSpec((1, d), lambda i, j: (0, 0)),
            pl.BlockSpec((d, tn), lambda i, j: (0, j)),
            pl.BlockSpec((d, LANES), lambda i, j: (0, 0)),
        ],
        out_specs=[pl.BlockSpec((tm, tn), lambda i, j: (i, j)), pl.BlockSpec((tm, LANES), lambda i, j: (i, 0))],
        out_shape=[jax.ShapeDtypeStruct((m, n), F32), jax.ShapeDtypeStruct((m, LANES), F32)],
        scratch_shapes=[pltpu.VMEM((tm, d), BF16)],
        compiler_params=_params(("arbitrary", "arbitrary")),
        name="rms_project",
    )(x, norm_w.reshape(1, d), w_main, w_small)


def _outproj_kernel(n_piece, *refs):
    x_ref, w_ref, out_ref = refs[n_piece], refs[n_piece + 1], refs[n_piece + 2]
    acc = x_ref[...]
    off = 0
    for p_ref in refs[:n_piece]:
        k = p_ref.shape[1]
        acc = acc + jnp.dot(p_ref[...].astype(BF16), w_ref[off:off + k, :], preferred_element_type=F32)
        off += k
    out_ref[...] = acc


def out_project(pieces, x, w, tm):
    m, d = x.shape
    in_specs = [pl.BlockSpec((tm, a.shape[1]), lambda i: (i, 0)) for a in pieces]
    in_specs += [pl.BlockSpec((tm, d), lambda i: (i, 0)), pl.BlockSpec(w.shape, lambda i: (0, 0))]
    return pl.pallas_call(
        functools.partial(_outproj_kernel, len(pieces)),
        grid=(m // tm,),
        in_specs=in_specs,
        out_specs=pl.BlockSpec((tm, d), lambda i: (i, 0)),
        out_shape=jax.ShapeDtypeStruct((m, d), F32),
        compiler_params=_params(("arbitrary",)),
        name="out_project",
    )(*pieces, x, w)


def _mlp_kernel(final, x_ref, nw_ref, wu_ref, wd_ref, nf_ref, out_ref, h_ref, acc_ref):
    j = pl.program_id(1)

    @pl.when(j == 0)
    def _():
        h_ref[...] = _rms(x_ref[...], nw_ref[...]).astype(BF16)
        acc_ref[...] = jnp.zeros_like(acc_ref)

    u = jnp.dot(h_ref[...], wu_ref[...], preferred_element_type=F32)
    u = jnp.maximum(u, 0.0)
    acc_ref[...] += jnp.dot((u * u).astype(BF16), wd_ref[...], preferred_element_type=F32)

    @pl.when(j == pl.num_programs(1) - 1)
    def _():
        y = x_ref[...] + acc_ref[...]
        if final:
            y = _rms(y, nf_ref[...])
        out_ref[...] = y


def mlp_block(x, norm_w, w_up, w_down, norm_f, final, tm, tf):
    m, d = x.shape
    dff = w_up.shape[1]
    return pl.pallas_call(
        functools.partial(_mlp_kernel, final),
        grid=(m // tm, dff // tf),
        in_specs=[
            pl.BlockSpec((tm, d), lambda i, j: (i, 0)),
            pl.BlockSpec((1, d), lambda i, j: (0, 0)),
            pl.BlockSpec((d, tf), lambda i, j: (0, j)),
            pl.BlockSpec((tf, d), lambda i, j: (j, 0)),
            pl.BlockSpec((1, d), lambda i, j: (0, 0)),
        ],
        out_specs=pl.BlockSpec((tm, d), lambda i, j: (i, 0)),
        out_shape=jax.ShapeDtypeStruct((m, d), F32),
        scratch_shapes=[pltpu.VMEM((tm, d), BF16), pltpu.VMEM((tm, d), F32)],
        compiler_params=_params(("arbitrary", "arbitrary")),
        name="mlp_block",
    )(x, norm_w.reshape(1, d), w_up, w_down, norm_f.reshape(1, d))


def _rms_norm(x, w):
    xf = x.astype(F32)
    y = xf * lax.rsqrt(jnp.mean(xf * xf, axis=-1, keepdims=True) + EPS)
    return (y * w.astype(F32)).astype(x.dtype)


def _group_rms_norm(x, w, groups):
    shp = x.shape
    xg = x.astype(F32).reshape(*shp[:-1], groups, shp[-1] // groups)
    xg = xg * lax.rsqrt(jnp.mean(xg * xg, axis=-1, keepdims=True) + EPS)
    return (xg.reshape(shp) * w.astype(F32)).astype(x.dtype)


def _l2_normalize(x):
    xf = x.astype(F32)
    return xf * lax.rsqrt(jnp.sum(xf * xf, axis=-1, keepdims=True) + 1e-6)


def _causal_conv(x, prefix, w):
    seq = x.shape[1]
    xp = jnp.concatenate([prefix.astype(x.dtype), x], axis=1)
    y = xp[:, :seq] * w[0]
    for tap in range(1, CONV_W):
        y = y + xp[:, tap:tap + seq] * w[tap]
    return y, xp[:, seq:]


def _band_attention(q, k, v):
    n, seq, heads, dh = q.shape
    nb = -(-seq // A_BAND)
    pad = nb * A_BAND - seq

    def padded(t, front):
        return jnp.pad(t, ((0, 0), (front, pad), (0, 0), (0, 0)))

    qb = padded(q, 0).reshape(n, nb, A_BAND, heads, dh)
    kb = padded(k, A_BAND).reshape(n, nb + 1, A_BAND, heads, dh)
    vb = padded(v, A_BAND).reshape(n, nb + 1, A_BAND, heads, dh)
    kb = jnp.concatenate([kb[:, :-1], kb[:, 1:]], axis=2)
    vb = jnp.concatenate([vb[:, :-1], vb[:, 1:]], axis=2)
    s = jnp.einsum('nbqhd,nbkhd->nbhqk', qb, kb, preferred_element_type=F32) * (A_HEAD_DIM ** -0.5)
    qi = jnp.arange(A_BAND)[:, None]
    kj = jnp.arange(2 * A_BAND)[None, :]
    dist = A_BAND + qi - kj
    blk = jnp.arange(nb)[:, None, None]
    valid = (dist >= 0) & (dist <= A_BAND) & (blk * A_BAND + kj >= A_BAND)
    s = jnp.where(valid[None, :, None], s, -jnp.inf)
    lse = jax.nn.logsumexp(s, axis=-1)
    p = jnp.exp(s - lse[..., None]).astype(v.dtype)
    o = jnp.einsum('nbhqk,nbkhd->nbqhd', p, vb).reshape(n, nb * A_BAND, heads, dh)[:, :seq]
    lse = lse.transpose(0, 1, 3, 2).reshape(n, nb * A_BAND, heads)[:, :seq]
    return o, lse


def _dilated_prompt(q, k, v, d):
    b, seq, heads, dh = q.shape

    def split(t):
        return t.reshape(b, seq // d, d, heads, dh).transpose(0, 2, 1, 3, 4).reshape(b * d, seq // d, heads, dh)

    o, lse = _band_attention(split(q), split(k), split(v))
    o = o.reshape(b, d, seq // d, heads, dh).transpose(0, 2, 1, 3, 4).reshape(b, seq, heads, dh)
    lse = lse.reshape(b, d, seq // d, heads).transpose(0, 2, 1, 3).reshape(b, seq, heads)
    return o, lse


def _dilated_sample(q, k_all, v_all, d, n_buf):
    seq = q.shape[1]
    idx = n_buf + jnp.arange(seq)[:, None] - d * jnp.arange(A_BAND + 1)[None, :]
    valid = idx >= 0
    idx = jnp.maximum(idx, 0)
    kg = k_all[:, idx]
    vg = v_all[:, idx]
    s = jnp.einsum('blhd,bljhd->blhj', q, kg, preferred_element_type=F32) * (A_HEAD_DIM ** -0.5)
    s = jnp.where(valid[None, :, None, :], s, -jnp.inf)
    lse = jax.nn.logsumexp(s, axis=-1)
    p = jnp.exp(s - lse[..., None]).astype(v_all.dtype)
    return jnp.einsum('blhj,bljhd->blhd', p, vg), lse


def _ssd_scan(x, dt, a, bm, cm, h0):
    b, seq, groups, hpg, pdim = x.shape
    n = bm.shape[-1]
    c = math.gcd(seq, SSD_CHUNK)
    nc = seq // c
    x = x.astype(F32).reshape(b, nc, c, groups, hpg, pdim)
    dt = dt.astype(F32).reshape(b, nc, c, groups, hpg)
    bm = bm.astype(F32).reshape(b, nc, c, groups, n)
    cm = cm.astype(F32).reshape(b, nc, c, groups, n)
    acs = jnp.cumsum(dt * a.astype(F32), axis=2)
    causal = jnp.tril(jnp.ones((c, c), bool))[:, :, None, None]
    seg = acs[:, :, :, None] - acs[:, :, None, :]
    decay = jnp.exp(jnp.where(causal, seg, -jnp.inf))
    cb = jnp.einsum('bctgn,bcsgn->bctsg', cm, bm)
    w_ts = cb[..., None] * decay * dt[:, :, None]
    y_diag = jnp.einsum('bctsgj,bcsgjp->bctgjp', w_ts, x)
    decay_end = jnp.exp(acs[:, :, -1:] - acs) * dt
    chunk_states = jnp.einsum('bcsgn,bcsgj,bcsgjp->bcgjpn', bm, decay_end, x)
    chunk_decay = jnp.exp(acs[:, :, -1])

    def step(h, inp):
        st, dec = inp
        return h * dec[..., None, None] + st, h

    h_final, h_prev = lax.scan(step, h0.astype(F32),
                               (jnp.moveaxis(chunk_states, 1, 0), jnp.moveaxis(chunk_decay, 1, 0)))
    h_prev = jnp.moveaxis(h_prev, 0, 1)
    y_off = jnp.einsum('bctgn,bcgjpn->bctgjp', cm, h_prev) * jnp.exp(acs)[..., None]
    return (y_diag + y_off).reshape(b, seq, groups, hpg, pdim), h_final


def _mamba2_mixer(z, xbc, dt_raw, conv_prefix, ssm0, conv_w, conv_b, dt_bias, a_log, d_skip, norm_w):
    b, seq, _ = xbc.shape
    xbc, conv_state = _causal_conv(xbc, conv_prefix, conv_w)
    xbc = jax.nn.silu(xbc + conv_b)
    xs, bm, cm = jnp.split(xbc, (B_INNER, B_INNER + B_GROUPS * B_STATE), axis=-1)
    xs = xs.reshape(b, seq, B_GROUPS, B_HPG, B_HEAD_DIM)
    bm = bm.reshape(b, seq, B_GROUPS, B_STATE)
    cm = cm.reshape(b, seq, B_GROUPS, B_STATE)
    dt = jax.nn.softplus(dt_raw.astype(F32) + dt_bias.astype(F32)).reshape(b, seq, B_GROUPS, B_HPG)
    a = -jnp.exp(a_log.astype(F32)).reshape(B_GROUPS, B_HPG)
    y, h = _ssd_scan(xs, dt, a, bm, cm, ssm0.reshape(b, B_GROUPS, B_HPG, B_HEAD_DIM, B_STATE))
    y = y + d_skip.astype(F32).reshape(B_GROUPS, B_HPG, 1) * xs.astype(F32)
    y = y.reshape(b, seq, B_INNER) * jax.nn.silu(z.astype(F32))
    y = _group_rms_norm(y, norm_w, B_GROUPS)
    return y.astype(z.dtype), conv_state, h.reshape(b, B_HEADS, B_HEAD_DIM, B_STATE)


def _gated_delta_chunked(q, k, v, log_alpha, beta, s0):
    b, seq, heads, dk = q.shape
    dv = v.shape[-1]
    c = math.gcd(seq, DELTA_CHUNK)
    nc = seq // c

    def blocks(t):
        return jnp.moveaxis(t.astype(F32).reshape(b, nc, c, heads, *t.shape[3:]), 3, 2)

    q, k, v = blocks(q), blocks(k), blocks(v)
    la, bt = blocks(log_alpha), blocks(beta)
    g = jnp.cumsum(la, axis=-1)
    seg = g[..., :, None] - g[..., None, :]
    idx = jnp.arange(c)
    strict = idx[:, None] > idx[None, :]
    incl = idx[:, None] >= idx[None, :]
    kk = jnp.einsum('bchtd,bchsd->bchts', k, k)
    qk = jnp.einsum('bchtd,bchsd->bchts', q, k)
    a_mat = bt[..., :, None] * jnp.exp(jnp.where(strict, seg, -jnp.inf)) * kk
    p_mat = jnp.exp(jnp.where(incl, seg, -jnp.inf)) * qk
    rhs = jnp.concatenate([bt[..., None] * v, (bt * jnp.exp(g))[..., None] * k], axis=-1)
    sol = lax.linalg.triangular_solve(a_mat + jnp.eye(c, dtype=F32), rhs, left_side=True, lower=True,
                                      unit_diagonal=True)
    w_v, w_k = sol[..., :dv], sol[..., dv:]
    k_end = jnp.exp(g[..., -1:] - g)[..., None] * k
    g_last = jnp.exp(g[..., -1])
    q_dec = jnp.exp(g)[..., None] * q

    def step(s, inp):
        wv, wk, qd, pm, ke, gl = inp
        u = wv - jnp.einsum('bhtk,bhkv->bhtv', wk, s)
        o = jnp.einsum('bhtk,bhkv->bhtv', qd, s) + jnp.einsum('bhts,bhsv->bhtv', pm, u)
        s = gl[..., None, None] * s + jnp.einsum('bhsk,bhsv->bhkv', ke, u)
        return s, o

    xs = tuple(jnp.moveaxis(t, 1, 0) for t in (w_v, w_k, q_dec, p_mat, k_end, g_last))
    s_final, o = lax.scan(step, s0.astype(F32), xs)
    o = jnp.moveaxis(jnp.moveaxis(o, 0, 1), 3, 2).reshape(b, seq, heads, dv)
    return o, s_final


def _gated_delta_mixer(qkv, z, a_raw, b_raw, conv_prefix, s0, conv_w, dt_bias, a_log, norm_w):
    b, seq, _ = qkv.shape
    qkv, conv_state = _causal_conv(qkv, conv_prefix, conv_w)
    qkv = jax.nn.silu(qkv)
    q, k, v = (t.reshape(b, seq, C_HEADS, C_HEAD_DIM) for t in jnp.split(qkv, 3, axis=-1))
    q = _l2_normalize(q) * (C_HEAD_DIM ** -0.5)
    k = _l2_normalize(k)
    log_alpha = -jnp.exp(a_log.astype(F32)) * jax.nn.softplus(a_raw.astype(F32) + dt_bias.astype(F32))
    beta = jax.nn.sigmoid(b_raw.astype(F32))
    o, s = _gated_delta_chunked(q, k, v, log_alpha, beta, s0)
    o = _rms_norm(o, norm_w) * jax.nn.silu(z.astype(F32)).reshape(b, seq, C_HEADS, C_HEAD_DIM)
    return o.reshape(b, seq, C_WIDTH).astype(z.dtype), conv_state, s


def _stick_breaking_weights(z, mask):
    z = z.astype(F32)
    log_keep = jnp.where(mask, jax.nn.log_sigmoid(-z), 0.0)
    after = lax.cumsum(log_keep, axis=z.ndim - 1, reverse=True) - log_keep
    return jnp.where(mask, jnp.exp(jax.nn.log_sigmoid(z) + after), 0.0)


def _stick_breaking_prompt(q, k, v, bias):
    b, seq, heads, dh = q.shape
    nq = seq // QBLK
    qb = q.reshape(b, nq, QBLK, heads, dh).transpose(1, 0, 2, 3, 4)
    kpos = jnp.arange(seq)
    bias = bias.astype(F32)[None, :, None, None]

    def block(args):
        qblk, i = args
        z = jnp.einsum('bqhd,bkhd->bhqk', qblk, k, preferred_element_type=F32) * (D_HEAD_DIM ** -0.5)
        qpos = i * QBLK + jnp.arange(QBLK)
        w = _stick_breaking_weights(z + bias, kpos[None, :] < qpos[:, None])
        return jnp.einsum('bhqk,bkhd->bqhd', w.astype(v.dtype), v)

    o = lax.map(block, (qb, jnp.arange(nq)))
    return o.transpose(1, 0, 2, 3, 4).reshape(b, seq, heads, dh)


def _stick_breaking_sample(q, k, v, k_past, v_past, bias):
    seq = q.shape[1]
    past = k_past.shape[1]
    z = jnp.concatenate([
        jnp.einsum('blhd,bkhd->bhlk', q, k_past.astype(q.dtype), preferred_element_type=F32),
        jnp.einsum('blhd,bkhd->bhlk', q, k, preferred_element_type=F32)], axis=-1) * (D_HEAD_DIM ** -0.5)
    z = z + bias.astype(F32)[None, :, None, None]
    mask = jnp.concatenate([jnp.ones((seq, past), bool), jnp.tril(jnp.ones((seq, seq), bool), -1)], axis=1)
    w = _stick_breaking_weights(z, mask).astype(v.dtype)
    return (jnp.einsum('bhlk,bkhd->blhd', w[..., :past], v_past.astype(v.dtype))
            + jnp.einsum('bhlk,bkhd->blhd', w[..., past:], v))


SB_BLK = 256
NT = (((1,), (1,)), ((), ()))


def _sb_kernel(q_ref, k_ref, v_ref, bias_ref, u_ref, o_ref, c_ref):
    i = pl.program_id(1)
    j = pl.program_id(2)

    @pl.when(j == 0)
    def _():
        o_ref[...] = jnp.zeros_like(o_ref)
        c_ref[...] = jnp.zeros_like(c_ref)

    @pl.when(j <= i)
    def _():
        row = lax.broadcasted_iota(jnp.int32, (SB_BLK, SB_BLK), 0)
        col = lax.broadcasted_iota(jnp.int32, (SB_BLK, SB_BLK), 1)
        mask = (i - j) * SB_BLK + col < i * SB_BLK + row
        half0 = lax.broadcasted_iota(jnp.int32, (SB_BLK, LANES), 1) < D_HEAD_DIM
        u = u_ref[...]
        for p in range(D_HEADS // 2):
            c0 = p * LANES
            q2 = q_ref[:, c0:c0 + LANES] * (D_HEAD_DIM ** -0.5)
            k2 = k_ref[:, c0:c0 + LANES].astype(BF16)
            v2 = v_ref[:, c0:c0 + LANES].astype(BF16)
            outs = []
            for hh in range(2):
                h = 2 * p + hh
                qm = jnp.where(half0 if hh == 0 else ~half0, q2, 0.0).astype(BF16)
                z = lax.dot_general(qm, k2, NT, preferred_element_type=F32) + bias_ref[h]
                sp = jnp.maximum(z, 0.0) + jnp.log1p(jnp.exp(-jnp.abs(z)))
                sp = jnp.where(mask, sp, 0.0)
                hi = sp.astype(BF16)
                lo = (sp - hi.astype(F32)).astype(BF16)
                aft = (jnp.dot(hi, u, preferred_element_type=F32) + jnp.dot(lo, u, preferred_element_type=F32))
                c = c_ref[h]
                wgt = jnp.where(mask, jnp.exp(z - sp - aft - c), 0.0)
                outs.append(jnp.dot(wgt.astype(BF16), v2, preferred_element_type=F32))
                c_ref[h] = c + aft[:, 0:1] + sp[:, 0:1]
            o_ref[:, c0:c0 + LANES] += jnp.where(half0, outs[0], outs[1])


def stick_breaking_prompt(proj, b, s, cq, ck, cv, bias):
    nq = s // SB_BLK
    assert s % SB_BLK == 0
    idx = jnp.arange(SB_BLK)
    u = (idx[:, None] > idx[None, :]).astype(BF16)
    bias_rows = jnp.broadcast_to(bias.astype(F32)[:, None, None], (D_HEADS, 1, SB_BLK))
    kv = lambda col: pl.BlockSpec((None, SB_BLK, D_WIDTH), lambda bi, i, j: (bi, jnp.maximum(i - j, 0), col))
    return pl.pallas_call(
        _sb_kernel,
        grid=(b, nq, nq),
        in_specs=[
            pl.BlockSpec((None, SB_BLK, D_WIDTH), lambda bi, i, j: (bi, i, cq)),
            kv(ck), kv(cv),
            pl.BlockSpec((D_HEADS, 1, SB_BLK), lambda bi, i, j: (0, 0, 0)),
            pl.BlockSpec((SB_BLK, SB_BLK), lambda bi, i, j: (0, 0)),
        ],
        out_specs=pl.BlockSpec((None, SB_BLK, D_WIDTH), lambda bi, i, j: (bi, i, 0)),
        out_shape=jax.ShapeDtypeStruct((b, s, D_WIDTH), F32),
        scratch_shapes=[pltpu.VMEM((D_HEADS, SB_BLK, 1), F32)],
        compiler_params=_params(("arbitrary", "arbitrary", "arbitrary")),
        name="stick_breaking_prompt",
    )(proj, proj, proj, bias_rows, u)


def _pad_lanes(w):
    return jnp.pad(w, ((0, 0), (0, LANES - w.shape[1]))).astype(BF16)


def _run_trunk(x3, w, past):
    b, seq, d = x3.shape
    m = b * seq
    tm = min(1024, m)
    tm_out = min(512, m)
    x = x3.reshape(m, d)
    new = {}

    w_in = w['w_in_ab'][0]
    proj, small = rms_project(x, w['norm_mix'][0], w_in[:, :AB_MAIN].astype(BF16), _pad_lanes(w_in[:, AB_MAIN:]),
                              1024, tm)
    proj = proj.reshape(b, seq, AB_MAIN)
    qkv = proj[..., :A_COLS].reshape(b, seq, A_GROUPS, 3, A_HEADS, A_HEAD_DIM)
    z = proj[..., A_COLS:A_COLS + B_INNER]
    xbc = proj[..., A_COLS + B_INNER:]
    dt_raw = small[:, :B_HEADS].reshape(b, seq, B_HEADS)
    if past is None:
        conv0 = jnp.zeros((b, CONV_W - 1, B_CONV_DIM), F32)
        ssm0 = jnp.zeros((b, B_HEADS, B_HEAD_DIM, B_STATE), F32)
    else:
        conv0, ssm0 = past['b_conv'][0], past['b_ssm'][0]
    outs, lses = [], []
    for g, dil in enumerate(A_DILATIONS):
        q, k, v = qkv[:, :, g, 0], qkv[:, :, g, 1], qkv[:, :, g, 2]
        if past is None:
            o, lse = _dilated_prompt(q, k, v, dil)
            keep = min(A_WINDOWS[g], seq)
            new['a%d' % g] = jnp.stack([k[:, seq - keep:], v[:, seq - keep:]], axis=1)[None]
        else:
            buf = past['a%d' % g][0]
            k_all = jnp.concatenate([buf[:, 0], k], axis=1)
            v_all = jnp.concatenate([buf[:, 1], v], axis=1)
            o, lse = _dilated_sample(q, k_all, v_all, dil, buf.shape[2])
            new['a%d' % g] = jnp.stack([k, v], axis=1)[None]
        outs.append(o)
        lses.append(lse)
    alpha = jax.nn.softmax(jnp.stack(lses), axis=0)
    y_a = jnp.einsum('gblh,gblhd->blhd', alpha, jnp.stack(outs)).reshape(m, A_WIDTH)
    y_b, conv_st, ssm_st = _mamba2_mixer(z, xbc, dt_raw, conv0, ssm0, w['conv_w_b'][0], w['conv_b_b'][0],
                                         w['dt_bias_b'][0], w['a_log_b'][0], w['d_skip_b'][0], w['norm_b'][0])
    new['b_conv'], new['b_ssm'] = conv_st[None], ssm_st[None]
    x = out_project([y_a, y_b.reshape(m, B_INNER)], x, w['w_out_ab'][0].astype(BF16), tm_out)
    x = mlp_block(x, w['norm_mlp'][0], w['w_up'][0].astype(BF16), w['w_down'][0].astype(BF16), w['norm_f'],
                  False, tm, 512)

    w_in = w['w_in_cd'][0]
    w_main = jnp.concatenate([w_in[:, :CD_MAIN], w_in[:, CD_D0:]], axis=1).astype(BF16)
    proj, small = rms_project(x, w['norm_mix'][1], w_main, _pad_lanes(w_in[:, CD_MAIN:CD_D0]), 512, tm)
    proj = proj.reshape(b, seq, CD_MAIN + 3 * D_WIDTH)
    qkv_c = proj[..., :3 * C_WIDTH]
    z_c = proj[..., 3 * C_WIDTH:CD_MAIN]
    q_d, k_d, v_d = (proj[..., CD_MAIN + i * D_WIDTH:CD_MAIN + (i + 1) * D_WIDTH].reshape(b, seq, D_HEADS, D_HEAD_DIM)
                     for i in range(3))
    a_raw = small[:, :C_HEADS].reshape(b, seq, C_HEADS)
    b_raw = small[:, C_HEADS:2 * C_HEADS].reshape(b, seq, C_HEADS)
    if past is None:
        conv0 = jnp.zeros((b, CONV_W - 1, 3 * C_WIDTH), F32)
        delta0 = jnp.zeros((b, C_HEADS, C_HEAD_DIM, C_HEAD_DIM), F32)
    else:
        conv0, delta0 = past['c_conv'][0], past['c_delta'][0]
    y_c, conv_st, delta_st = _gated_delta_mixer(qkv_c, z_c, a_raw, b_raw, conv0, delta0, w['conv_w_c'][0],
                                                w['dt_bias_c'][0], w['a_log_c'][0], w['norm_c'][0])
    if past is None:
        cq = CD_MAIN // D_WIDTH
        y_d = stick_breaking_prompt(proj, b, seq, cq, cq + 1, cq + 2, w['sb_bias_d'][0])
    else:
        pt = past['page_table']
        k_past = past['d_k'][0][pt].reshape(b, -1, D_HEADS, D_HEAD_DIM)
        v_past = past['d_v'][0][pt].reshape(b, -1, D_HEADS, D_HEAD_DIM)
        y_d = _stick_breaking_sample(q_d, k_d, v_d, k_past, v_past, w['sb_bias_d'][0])
    new['c_conv'], new['c_delta'] = conv_st[None], delta_st[None]
    new['d_k'], new['d_v'] = k_d[None], v_d[None]
    x = out_project([y_c.reshape(m, C_WIDTH), y_d.reshape(m, D_WIDTH)], x, w['w_out_cd'][0].astype(BF16), tm_out)
    x = mlp_block(x, w['norm_mlp'][1], w['w_up'][1].astype(BF16), w['w_down'][1].astype(BF16), w['norm_f'],
                  True, tm, 512)
    return x.reshape(b, seq, d), new


def kernel(x_prompt, x_sample, cache_a0, cache_a1, cache_a2, state_b_conv, state_b_ssm, state_c_conv,
           state_c_delta, cache_d_k, cache_d_v, page_table, w_in_ab, conv_w_b, conv_b_b, dt_bias_b, a_log_b,
           d_skip_b, norm_b, w_out_ab, w_in_cd, conv_w_c, dt_bias_c, a_log_c, norm_c, w_out_cd, sb_bias_d,
           norm_mix, norm_mlp, w_up, w_down, norm_f):
    assert w_in_ab.shape[0] == 1 and w_in_cd.shape[0] == 1 and w_up.shape[0] == 2
    weights = {
        'w_in_ab': w_in_ab, 'conv_w_b': conv_w_b, 'conv_b_b': conv_b_b, 'dt_bias_b': dt_bias_b,
        'a_log_b': a_log_b, 'd_skip_b': d_skip_b, 'norm_b': norm_b, 'w_out_ab': w_out_ab,
        'w_in_cd': w_in_cd, 'conv_w_c': conv_w_c, 'dt_bias_c': dt_bias_c, 'a_log_c': a_log_c,
        'norm_c': norm_c, 'w_out_cd': w_out_cd, 'sb_bias_d': sb_bias_d, 'norm_mix': norm_mix,
        'norm_mlp': norm_mlp, 'w_up': w_up, 'w_down': w_down, 'norm_f': norm_f,
    }
    past = {
        'a0': cache_a0, 'a1': cache_a1, 'a2': cache_a2, 'b_conv': state_b_conv, 'b_ssm': state_b_ssm,
        'c_conv': state_c_conv, 'c_delta': state_c_delta, 'd_k': cache_d_k, 'd_v': cache_d_v,
        'page_table': page_table,
    }
    y_prompt, sp = _run_trunk(x_prompt, weights, None)
    y_sample, ss = _run_trunk(x_sample, weights, past)
    return (y_prompt, y_sample,
            sp['a0'], ss['a0'], sp['a1'], ss['a1'], sp['a2'], ss['a2'],
            sp['b_conv'], ss['b_conv'], sp['b_ssm'], ss['b_ssm'],
            sp['c_conv'], ss['c_conv'], sp['c_delta'], ss['c_delta'],
            sp['d_k'], ss['d_k'], sp['d_v'], ss['d_v'])
```
